```python
import jax, jax.numpy as jnp
from jax import lax
import numpy as np

D_MODEL = 1024
BATCH = 8
SEQ = 2048
DEPTH = 2
DEC_BATCH = 128
DEC_SEQ = 8
PAST_LEN = 16384
PAGE_SIZE = 128

D_MIX = D_MODEL
D_A = D_MIX // 2
A_HEADS = 8
A_HEAD_DIM = D_A // A_HEADS
CHUNK = 128
D_B = D_MIX // 4
CONV_B = 31
D_C = D_MIX - D_A - D_B
CONV_C = 3
D_IN = 2 * D_A + 2 * D_B + 3 * D_C
D_FF = 4 * D_MODEL
EPS = 1e-6

kernel_name = "hybrid_chunkmlp_conformerconv_shortconv_step"


def rmsnorm(x, g):
    xf = x.astype(jnp.float32)
    y = xf * lax.rsqrt(jnp.mean(xf * xf, axis=-1, keepdims=True) + EPS)
    return (y * g.astype(jnp.float32)).astype(x.dtype)


def layernorm(x, g, b):
    xf = x.astype(jnp.float32)
    mu = jnp.mean(xf, axis=-1, keepdims=True)
    var = jnp.mean(jnp.square(xf - mu), axis=-1, keepdims=True)
    y = (xf - mu) * lax.rsqrt(var + EPS)
    return (y * g.astype(jnp.float32) + b.astype(jnp.float32)).astype(x.dtype)


def causal_dwconv(xpad, w):
    return lax.conv_general_dilated(
        xpad, w[:, None, :].astype(xpad.dtype), window_strides=(1,), padding='VALID',
        dimension_numbers=('NWC', 'WIO', 'NWC'), feature_group_count=xpad.shape[-1])


def spatial_prompt(v, ws_tril, bias):
    n, t, _ = v.shape
    vc = v.reshape(n, t // CHUNK, CHUNK, A_HEADS, A_HEAD_DIM)
    mix = jnp.einsum('hts,ncshd->ncthd', ws_tril, vc) + bias.T[None, None, :, :, None]
    return mix.reshape(n, t, D_A)


def spatial_sample(v, ws_tril, bias):
    n, t, _ = v.shape
    p0 = PAST_LEN % CHUNK
    w = ws_tril[:, p0:p0 + t, p0:p0 + t]
    bb = bias[:, p0:p0 + t]
    vh = v.reshape(n, t, A_HEADS, A_HEAD_DIM)
    mix = jnp.einsum('hts,nshd->nthd', w, vh) + bb.T[None, :, :, None]
    return mix.reshape(n, t, D_A)


def mixer_block(h, hist_b, hist_c, spatial_fn, w_in, a_ln_g, a_ln_b, ws_tril, a_bias,
                b_conv_w, b_conv_b, b_ln_g, b_ln_b, c_conv_w, w_out):
    z = h @ w_in
    s = np.cumsum([D_A, D_A, D_B, D_B, D_C, D_C])
    za_u, za_v, zb_a, zb_g, zc_b, zc_c, zc_x = jnp.split(z, list(s), axis=-1)
    u = jax.nn.gelu(za_u)
    v = layernorm(jax.nn.gelu(za_v), a_ln_g, a_ln_b)
    ya = u * spatial_fn(v, ws_tril, a_bias)
    glu = zb_a * jax.nn.sigmoid(zb_g)
    xb = jnp.concatenate([hist_b, glu], axis=1)
    yb = causal_dwconv(xb, b_conv_w) + b_conv_b
    yb = jax.nn.silu(layernorm(yb, b_ln_g, b_ln_b))
    cx = zc_c * zc_x
    xc = jnp.concatenate([hist_c, cx], axis=1)
    yc = zc_b * causal_dwconv(xc, c_conv_w)
    y = jnp.concatenate([ya, yb, yc], axis=-1) @ w_out
    return (y, xb[:, xb.shape[1] - (CONV_B - 1):], xc[:, xc.shape[1] - (CONV_C - 1):], v)


def setup_inputs(seed: int = 0) -> dict:
    key = jax.random.key(seed)
    ks = jax.random.split(key, 24)
    nrm = lambda k, shp, sc: jax.random.normal(k, shp, jnp.float32) * sc
    return {
        "x_prompt": nrm(ks[0], (BATCH, SEQ, D_MODEL), 1.0),
        "x_sample": nrm(ks[1], (DEC_BATCH, DEC_SEQ, D_MODEL), 1.0),
        "state_conv_b": nrm(ks[2], (DEPTH, DEC_BATCH, CONV_B - 1, D_B), 0.5),
        "state_conv_c": nrm(ks[3], (DEPTH, DEC_BATCH, CONV_C - 1, D_C), 0.5),
        "norm1_g": 1.0 + nrm(ks[4], (DEPTH, D_MODEL), 0.02),
        "w_in": nrm(ks[5], (DEPTH, D_MODEL, D_IN), D_MODEL ** -0.5),
        "a_ln_g": 1.0 + nrm(ks[6], (DEPTH, D_A), 0.02),
        "a_ln_b": nrm(ks[7], (DEPTH, D_A), 0.02),
        "a_ws": nrm(ks[8], (DEPTH, A_HEADS, CHUNK, CHUNK), CHUNK ** -0.5),
        "a_bias": 1.0 + nrm(ks[9], (DEPTH, A_HEADS, CHUNK), 0.02),
        "b_conv_w": nrm(ks[10], (DEPTH, CONV_B, D_B), CONV_B ** -0.5),
        "b_conv_b": nrm(ks[11], (DEPTH, D_B), 0.02),
        "b_ln_g": 1.0 + nrm(ks[12], (DEPTH, D_B), 0.02),
        "b_ln_b": nrm(ks[13], (DEPTH, D_B), 0.02),
        "c_conv_w": nrm(ks[14], (DEPTH, CONV_C, D_C), CONV_C ** -0.5),
        "w_out": nrm(ks[15], (DEPTH, D_MIX, D_MODEL), D_MIX ** -0.5),
        "norm2_g": 1.0 + nrm(ks[16], (DEPTH, D_MODEL), 0.02),
        "w_ff1": nrm(ks[17], (DEPTH, D_MODEL, D_FF), D_MODEL ** -0.5),
        "w_ff2": nrm(ks[18], (DEPTH, D_FF, D_MODEL), D_FF ** -0.5),
        "norm_f_g": 1.0 + nrm(ks[19], (D_MODEL,), 0.02),
    }


def reference(x_prompt, x_sample, state_conv_b, state_conv_c, norm1_g, w_in, a_ln_g, a_ln_b,
              a_ws, a_bias, b_conv_w, b_conv_b, b_ln_g, b_ln_b, c_conv_w, w_out, norm2_g,
              w_ff1, w_ff2, norm_f_g):
    xp, xs = x_prompt, x_sample
    cb_p, cc_p, cb_s, cc_s, v_s = [], [], [], [], []
    for l in range(DEPTH):
        ws_tril = jnp.tril(a_ws[l])
        lw = (w_in[l], a_ln_g[l], a_ln_b[l], ws_tril, a_bias[l], b_conv_w[l], b_conv_b[l],
              b_ln_g[l], b_ln_b[l], c_conv_w[l], w_out[l])
        hp = rmsnorm(xp, norm1_g[l])
        zb = jnp.zeros((xp.shape[0], CONV_B - 1, D_B), xp.dtype)
        zc = jnp.zeros((xp.shape[0], CONV_C - 1, D_C), xp.dtype)
        yp, nbp, ncp, _ = mixer_block(hp, zb, zc, spatial_prompt, *lw)
        xp = xp + yp
        hp = rmsnorm(xp, norm2_g[l])
        xp = xp + jnp.square(jax.nn.relu(hp @ w_ff1[l])) @ w_ff2[l]
        hs = rmsnorm(xs, norm1_g[l])
        ys, nbs, ncs, vs = mixer_block(hs, state_conv_b[l], state_conv_c[l], spatial_sample, *lw)
        xs = xs + ys
        hs = rmsnorm(xs, norm2_g[l])
        xs = xs + jnp.square(jax.nn.relu(hs @ w_ff1[l])) @ w_ff2[l]
        cb_p.append(nbp); cc_p.append(ncp); cb_s.append(nbs); cc_s.append(ncs); v_s.append(vs)
    y_prompt = rmsnorm(xp, norm_f_g)
    y_sample = rmsnorm(xs, norm_f_g)
    new_conv_b_prompt = jnp.stack(cb_p)
    new_conv_c_prompt = jnp.stack(cc_p)
    new_conv_b_sample = jnp.stack(cb_s)
    new_conv_c_sample = jnp.stack(cc_s)
    new_chunk_v_sample = jnp.stack(v_s)
    return (y_prompt, y_sample, new_conv_b_prompt, new_conv_c_prompt, new_conv_b_sample, new_conv_c_sample, new_chunk_v_sample)
```

```python
import functools

import jax
import jax.numpy as jnp
from jax import lax
from jax.experimental import pallas as pl
from jax.experimental.pallas import tpu as pltpu

D_MODEL = 1024
D_A = 512
A_HEADS = 8
A_HEAD_DIM = D_A // A_HEADS
CHUNK = 128
D_B = 256
CONV_B = 31
D_C = 256
CONV_C = 3
D_IN = 2 * D_A + 2 * D_B + 3 * D_C
D_FF = 4 * D_MODEL
EPS = 1e-6

SUBLANES = 8
LANES = 128
HEAD_PAIRS = D_A // LANES
VMEM_LIMIT_BYTES = 56 * 1024 * 1024

ROW_TILE = 512
FF_CHUNK = 1024
CONV_ROWS = 64
HIST_B_PAD = 32
HIST_C_PAD = 8
SAMPLE_T = 8

_F32 = jnp.float32
_BF16 = jnp.bfloat16


def _rmsnorm(x, g):
    return x * lax.rsqrt(jnp.mean(x * x, axis=-1, keepdims=True) + EPS) * g


def _layernorm(x, g, b):
    mu = jnp.mean(x, axis=-1, keepdims=True)
    xc = x - mu
    var = jnp.mean(xc * xc, axis=-1, keepdims=True)
    return xc * lax.rsqrt(var + EPS) * g + b


def _mixer_kernel(sample, ts, *refs):
    if sample:
        (x_ref, n1g_ref, win_ref, alng_ref, alnb_ref, aw_ref, abias_ref, bcw_ref, bcb_ref,
         blng_ref, blnb_ref, ccw_ref, wout_ref, hb_ref, hc_ref,
         out_ref, newb_ref, newc_ref, v_ref, wp_scr, sb_scr, sc_scr) = refs
        first = pl.program_id(0) == 0
    else:
        (x_ref, n1g_ref, win_ref, alng_ref, alnb_ref, aw_ref, abias_ref, bcw_ref, bcb_ref,
         blng_ref, blnb_ref, ccw_ref, wout_ref,
         out_ref, newb_ref, newc_ref, wp_scr, sb_scr, sc_scr) = refs
        seq_tile = pl.program_id(1)
        first = jnp.logical_and(pl.program_id(0) == 0, seq_tile == 0)

    @pl.when(first)
    def _():
        row = lax.broadcasted_iota(jnp.int32, (CHUNK, CHUNK), 0)
        col = lax.broadcasted_iota(jnp.int32, (CHUNK, CHUNK), 1)
        keep = row >= col
        if sample:
            keep = jnp.logical_and(keep, (row // SAMPLE_T) == (col // SAMPLE_T))
        for p in range(HEAD_PAIRS):
            w0 = jnp.where(keep, aw_ref[2 * p], 0.0)
            w1 = jnp.where(keep, aw_ref[2 * p + 1], 0.0)
            wp_scr[p] = jnp.concatenate([w0, w1], axis=1).astype(_BF16)

    x = x_ref[...].reshape(ts, D_MODEL)
    h = _rmsnorm(x, n1g_ref[...]).astype(_BF16)
    z = jnp.dot(h, win_ref[...], preferred_element_type=_F32)

    o = 0
    za_u = z[:, o:o + D_A]; o += D_A
    za_v = z[:, o:o + D_A]; o += D_A
    zb_a = z[:, o:o + D_B]; o += D_B
    zb_g = z[:, o:o + D_B]; o += D_B
    zc_b = z[:, o:o + D_C]; o += D_C
    zc_c = z[:, o:o + D_C]; o += D_C
    zc_x = z[:, o:o + D_C]

    u = jax.nn.gelu(za_u)
    v = _layernorm(jax.nn.gelu(za_v), alng_ref[...], alnb_ref[...])
    if sample:
        v_ref[...] = v
    lane = lax.broadcasted_iota(jnp.int32, (ts, D_A), 1)
    low_head = (lane % LANES) < A_HEAD_DIM
    v_lo = jnp.where(low_head, v, 0.0).astype(_BF16)
    v_hi = jnp.where(low_head, 0.0, v).astype(_BF16)
    bias = abias_ref[...]
    ya_chunks = []
    for c in range(ts // CHUNK):
        rows = slice(c * CHUNK, (c + 1) * CHUNK)
        mix = []
        for p in range(HEAD_PAIRS):
            cols = slice(p * LANES, (p + 1) * LANES)
            rhs = jnp.concatenate([v_lo[rows, cols], v_hi[rows, cols]], axis=0)
            mix.append(jnp.dot(wp_scr[p], rhs, preferred_element_type=_F32))
        ya_chunks.append(u[rows] * (jnp.concatenate(mix, axis=1) + bias))
    ya = jnp.concatenate(ya_chunks, axis=0)

    glu = zb_a * jax.nn.sigmoid(zb_g)
    cx = zc_c * zc_x
    if sample:
        nseq = ts // SAMPLE_T
        sb_scr[:, 0:CONV_B - 1, :] = hb_ref[...]
        sb_scr[:, CONV_B - 1:CONV_B - 1 + SAMPLE_T, :] = glu.reshape(nseq, SAMPLE_T, D_B)
        sc_scr[:, 0:CONV_C - 1, :] = hc_ref[...]
        sc_scr[:, CONV_C - 1:CONV_C - 1 + SAMPLE_T, :] = cx.reshape(nseq, SAMPLE_T, D_C)
        acc = jnp.zeros((nseq, SAMPLE_T, D_B), _F32)
        for k in range(CONV_B):
            acc = acc + bcw_ref[pl.ds(k, 1), :] * sb_scr[:, k:k + SAMPLE_T, :]
        yb = acc.reshape(ts, D_B) + bcb_ref[...]
        accc = jnp.zeros((nseq, SAMPLE_T, D_C), _F32)
        for k in range(CONV_C):
            accc = accc + ccw_ref[pl.ds(k, 1), :] * sc_scr[:, k:k + SAMPLE_T, :]
        conv_c = accc.reshape(ts, D_C)
        newb_ref[...] = sb_scr[:, SAMPLE_T:SAMPLE_T + CONV_B - 1, :]
        newc_ref[...] = sc_scr[:, SAMPLE_T:SAMPLE_T + CONV_C - 1, :]
    else:
        @pl.when(seq_tile == 0)
        def _():
            sb_scr[0:HIST_B_PAD, :] = jnp.zeros((HIST_B_PAD, D_B), _F32)
            sc_scr[0:HIST_C_PAD, :] = jnp.zeros((HIST_C_PAD, D_C), _F32)

        sb_scr[HIST_B_PAD:HIST_B_PAD + ts, :] = glu
        sc_scr[HIST_C_PAD:HIST_C_PAD + ts, :] = cx
        off_b = HIST_B_PAD - (CONV_B - 1)
        off_c = HIST_C_PAD - (CONV_C - 1)
        yb_chunks = []
        cc_chunks = []
        for r0 in range(0, ts, CONV_ROWS):
            acc = jnp.zeros((CONV_ROWS, D_B), _F32)
            for k in range(CONV_B):
                acc = acc + bcw_ref[pl.ds(k, 1), :] * sb_scr[pl.ds(r0 + k + off_b, CONV_ROWS), :]
            yb_chunks.append(acc)
            accc = jnp.zeros((CONV_ROWS, D_C), _F32)
            for k in range(CONV_C):
                accc = accc + ccw_ref[pl.ds(k, 1), :] * sc_scr[pl.ds(r0 + k + off_c, CONV_ROWS), :]
            cc_chunks.append(accc)
        yb = jnp.concatenate(yb_chunks, axis=0) + bcb_ref[...]
        conv_c = jnp.concatenate(cc_chunks, axis=0)

        newb_ref[0] = sb_scr[pl.ds(HIST_B_PAD + ts - (CONV_B - 1), CONV_B - 1), :]
        newc_ref[0] = sc_scr[pl.ds(HIST_C_PAD + ts - (CONV_C - 1), CONV_C - 1), :]
        sb_scr[0:HIST_B_PAD, :] = sb_scr[ts:ts + HIST_B_PAD, :]
        sc_scr[0:HIST_C_PAD, :] = sc_scr[ts:ts + HIST_C_PAD, :]

    yb = jax.nn.silu(_layernorm(yb, blng_ref[...], blnb_ref[...]))
    yc = zc_b * conv_c

    ymix = jnp.concatenate([ya, yb, yc], axis=-1).astype(_BF16)
    y = jnp.dot(ymix, wout_ref[...], preferred_element_type=_F32)
    out_ref[...] = (x + y).reshape(out_ref.shape)


def _const_spec(shape):
    nd = len(shape)
    return pl.BlockSpec(shape, lambda *_: (0,) * nd, pipeline_mode=pl.Buffered(1))


def _mixer_weight_specs():
    return [
        _const_spec((1, D_MODEL)),
        _const_spec((D_MODEL, D_IN)),
        _const_spec((1, D_A)),
        _const_spec((1, D_A)),
        _const_spec((A_HEADS, CHUNK, CHUNK)),
        _const_spec((CHUNK, D_A)),
        _const_spec((CONV_B, D_B)),
        _const_spec((1, D_B)),
        _const_spec((1, D_B)),
        _const_spec((1, D_B)),
        _const_spec((CONV_C, D_C)),
        _const_spec((D_MODEL, D_MODEL)),
    ]


def _mixer_prompt(x, lw):
    batch, seq, _ = x.shape
    ts = ROW_TILE
    grid = (batch, seq // ts)
    return pl.pallas_call(
        functools.partial(_mixer_kernel, False, ts),
        grid=grid,
        in_specs=[pl.BlockSpec((1, ts, D_MODEL), lambda b, s: (b, s, 0))] + _mixer_weight_specs(),
        out_specs=[
            pl.BlockSpec((1, ts, D_MODEL), lambda b, s: (b, s, 0)),
            pl.BlockSpec((1, CONV_B - 1, D_B), lambda b, s: (b, 0, 0)),
            pl.BlockSpec((1, CONV_C - 1, D_C), lambda b, s: (b, 0, 0)),
        ],
        out_shape=[
            jax.ShapeDtypeStruct((batch, seq, D_MODEL), _F32),
            jax.ShapeDtypeStruct((batch, CONV_B - 1, D_B), _F32),
            jax.ShapeDtypeStruct((batch, CONV_C - 1, D_C), _F32),
        ],
        scratch_shapes=[
            pltpu.VMEM((HEAD_PAIRS, CHUNK, 2 * CHUNK), _BF16),
            pltpu.VMEM((HIST_B_PAD + ts, D_B), _F32),
            pltpu.VMEM((HIST_C_PAD + ts, D_C), _F32),
        ],
        compiler_params=pltpu.CompilerParams(
            dimension_semantics=("arbitrary", "arbitrary"),
            vmem_limit_bytes=VMEM_LIMIT_BYTES),
        name="mixer_prompt",
    )(x, *lw)


def _mixer_sample(x, lw, hist_b, hist_c):
    rows = x.shape[0]
    ts = ROW_TILE
    nseq = ts // SAMPLE_T
    grid = (rows // ts,)
    return pl.pallas_call(
        functools.partial(_mixer_kernel, True, ts),
        grid=grid,
        in_specs=[pl.BlockSpec((ts, D_MODEL), lambda i: (i, 0))] + _mixer_weight_specs() + [
            pl.BlockSpec((nseq, CONV_B - 1, D_B), lambda i: (i, 0, 0)),
            pl.BlockSpec((nseq, CONV_C - 1, D_C), lambda i: (i, 0, 0)),
        ],
        out_specs=[
            pl.BlockSpec((ts, D_MODEL), lambda i: (i, 0)),
            pl.BlockSpec((nseq, CONV_B - 1, D_B), lambda i: (i, 0, 0)),
            pl.BlockSpec((nseq, CONV_C - 1, D_C), lambda i: (i, 0, 0)),
            pl.BlockSpec((ts, D_A), lambda i: (i, 0)),
        ],
        out_shape=[
            jax.ShapeDtypeStruct((rows, D_MODEL), _F32),
            jax.ShapeDtypeStruct((rows // SAMPLE_T, CONV_B - 1, D_B), _F32),
            jax.ShapeDtypeStruct((rows // SAMPLE_T, CONV_C - 1, D_C), _F32),
            jax.ShapeDtypeStruct((rows, D_A), _F32),
        ],
        scratch_shapes=[
            pltpu.VMEM((HEAD_PAIRS, CHUNK, 2 * CHUNK), _BF16),
            pltpu.VMEM((nseq, HIST_B_PAD + SAMPLE_T, D_B), _F32),
            pltpu.VMEM((nseq, HIST_C_PAD + SAMPLE_T, D_C), _F32),
        ],
        compiler_params=pltpu.CompilerParams(
            dimension_semantics=("arbitrary",),
            vmem_limit_bytes=VMEM_LIMIT_BYTES),
        name="mixer_sample",
    )(x, *lw, hist_b, hist_c)


def _ffn_kernel(final, x_ref, g_ref, w1_ref, w2_ref, gf_ref, out_ref):
    x = x_ref[...]
    h = _rmsnorm(x, g_ref[...]).astype(_BF16)
    acc = x
    for j in range(D_FF // FF_CHUNK):
        cols = slice(j * FF_CHUNK, (j + 1) * FF_CHUNK)
        a = jnp.dot(h, w1_ref[:, cols], preferred_element_type=_F32)
        a = jnp.square(jnp.maximum(a, 0.0)).astype(_BF16)
        acc = acc + jnp.dot(a, w2_ref[cols, :], preferred_element_type=_F32)
    if final:
        acc = _rmsnorm(acc, gf_ref[...])
    out_ref[...] = acc


def _ffn(x, g, w1, w2, gf, final):
    rows = x.shape[0]
    tm = ROW_TILE
    return pl.pallas_call(
        functools.partial(_ffn_kernel, final),
        grid=(rows // tm,),
        in_specs=[
            pl.BlockSpec((tm, D_MODEL), lambda i: (i, 0)),
            _const_spec((1, D_MODEL)),
            _const_spec((D_MODEL, D_FF)),
            _const_spec((D_FF, D_MODEL)),
            _const_spec((1, D_MODEL)),
        ],
        out_specs=pl.BlockSpec((tm, D_MODEL), lambda i: (i, 0)),
        out_shape=jax.ShapeDtypeStruct((rows, D_MODEL), _F32),
        compiler_params=pltpu.CompilerParams(
            dimension_semantics=("arbitrary",),
            vmem_limit_bytes=VMEM_LIMIT_BYTES),
        name="ffn_final" if final else "ffn",
    )(x, g, w1, w2, gf)


def kernel(x_prompt, x_sample, state_conv_b, state_conv_c, norm1_g, w_in, a_ln_g, a_ln_b, a_ws,
           a_bias, b_conv_w, b_conv_b, b_ln_g, b_ln_b, c_conv_w, w_out, norm2_g, w_ff1, w_ff2,
           norm_f_g):
    depth = w_in.shape[0]
    batch, seq, _ = x_prompt.shape
    dec_batch, dec_seq, _ = x_sample.shape
    assert dec_seq == SAMPLE_T and seq % ROW_TILE == 0 and (dec_batch * dec_seq) % ROW_TILE == 0

    xp = x_prompt
    xs = x_sample.reshape(dec_batch * dec_seq, D_MODEL)
    gf = norm_f_g.reshape(1, D_MODEL)
    reps = CHUNK // SAMPLE_T
    cb_p, cc_p, cb_s, cc_s, v_s = [], [], [], [], []
    for l in range(depth):
        final = l == depth - 1
        common = (
            norm1_g[l].reshape(1, D_MODEL),
            w_in[l].astype(_BF16),
            a_ln_g[l].reshape(1, D_A),
            a_ln_b[l].reshape(1, D_A),
        )
        tail = (
            b_conv_w[l],
            b_conv_b[l].reshape(1, D_B),
            b_ln_g[l].reshape(1, D_B),
            b_ln_b[l].reshape(1, D_B),
            c_conv_w[l],
            w_out[l].astype(_BF16),
        )
        bias_p = jnp.repeat(a_bias[l].T, A_HEAD_DIM, axis=1)
        bias_s = jnp.tile(bias_p[:SAMPLE_T], (reps, 1))
        aw_p = a_ws[l]
        aw_s = jnp.tile(a_ws[l][:, :SAMPLE_T, :SAMPLE_T], (1, reps, reps))
        g2 = norm2_g[l].reshape(1, D_MODEL)
        w1 = w_ff1[l].astype(_BF16)
        w2 = w_ff2[l].astype(_BF16)

        xp, nbp, ncp = _mixer_prompt(xp, common + (aw_p, bias_p) + tail)
        xp = _ffn(xp.reshape(batch * seq, D_MODEL), g2, w1, w2, gf, final).reshape(batch, seq, D_MODEL)
        xs, nbs, ncs, vs = _mixer_sample(xs, common + (aw_s, bias_s) + tail,
                                         state_conv_b[l], state_conv_c[l])
        xs = _ffn(xs, g2, w1, w2, gf, final)
        cb_p.append(nbp); cc_p.append(ncp); cb_s.append(nbs); cc_s.append(ncs)
        v_s.append(vs.reshape(dec_batch, dec_seq, D_A))

    return (xp, xs.reshape(dec_batch, dec_seq, D_MODEL), jnp.stack(cb_p), jnp.stack(cc_p),
            jnp.stack(cb_s), jnp.stack(cc_s), jnp.stack(v_s))
```

```python
import functools

import jax
import jax.numpy as jnp
from jax import lax
from jax.experimental import pallas as pl
from jax.experimental.pallas import tpu as pltpu

D_MODEL = 1024
D_A = 512
A_HEADS = 8
A_HEAD_DIM = D_A // A_HEADS
CHUNK = 128
D_B = 256
CONV_B = 31
D_C = 256
CONV_C = 3
D_IN = 2 * D_A + 2 * D_B + 3 * D_C
D_FF = 4 * D_MODEL
EPS = 1e-6

SUBLANES = 8
LANES = 128
HEAD_PAIRS = D_A // LANES
VMEM_LIMIT_BYTES = 56 * 1024 * 1024

ROW_TILE = 512
FF_CHUNK = 1024
CONV_ROWS = 64
HIST_B_PAD = 32
HIST_C_PAD = 8
SAMPLE_T = 8

_F32 = jnp.float32
_BF16 = jnp.bfloat16


def _rmsnorm(x, g):
    return x * lax.rsqrt(jnp.mean(x * x, axis=-1, keepdims=True) + EPS) * g


def _layernorm(x, g, b):
    mu = jnp.mean(x, axis=-1, keepdims=True)
    xc = x - mu
    var = jnp.mean(xc * xc, axis=-1, keepdims=True)
    return xc * lax.rsqrt(var + EPS) * g + b


def _mixer_kernel(sample, ts, *refs):
    if sample:
        (x_ref, n1g_ref, win_ref, alng_ref, alnb_ref, aw_ref, abias_ref, bcw_ref, bcb_ref,
         blng_ref, blnb_ref, ccw_ref, wout_ref, hb_ref, hc_ref,
         out_ref, newb_ref, newc_ref, v_ref, wp_scr, sb_scr, sc_scr) = refs
        first = pl.program_id(0) == 0
    else:
        (x_ref, n1g_ref, win_ref, alng_ref, alnb_ref, aw_ref, abias_ref, bcw_ref, bcb_ref,
         blng_ref, blnb_ref, ccw_ref, wout_ref,
         out_ref, newb_ref, newc_ref, wp_scr, sb_scr, sc_scr) = refs
        seq_tile = pl.program_id(1)
        first = jnp.logical_and(pl.program_id(0) == 0, seq_tile == 0)

    @pl.when(first)
    def _():
        row = lax.broadcasted_iota(jnp.int32, (CHUNK, CHUNK), 0)
        col = lax.broadcasted_iota(jnp.int32, (CHUNK, CHUNK), 1)
        keep = row >= col
        if sample:
            keep = jnp.logical_and(keep, (row // SAMPLE_T) == (col // SAMPLE_T))
        for p in range(HEAD_PAIRS):
            w0 = jnp.where(keep, aw_ref[2 * p], 0.0)
            w1 = jnp.where(keep, aw_ref[2 * p + 1], 0.0)
            wp_scr[p] = jnp.concatenate([w0, w1], axis=1).astype(_BF16)

    x = x_ref[...].reshape(ts, D_MODEL)
    h = _rmsnorm(x, n1g_ref[...]).astype(_BF16)
    z = jnp.dot(h, win_ref[...], preferred_element_type=_F32)

    o = 0
    za_u = z[:, o:o + D_A]; o += D_A
    za_v = z[:, o:o + D_A]; o += D_A
    zb_a = z[:, o:o + D_B]; o += D_B
    zb_g = z[:, o:o + D_B]; o += D_B
    zc_b = z[:, o:o + D_C]; o += D_C
    zc_c = z[:, o:o + D_C]; o += D_C
    zc_x = z[:, o:o + D_C]

    u = jax.nn.gelu(za_u)
    v = _layernorm(jax.nn.gelu(za_v), alng_ref[...], alnb_ref[...])
    if sample:
        v_ref[...] = v
    lane = lax.broadcasted_iota(jnp.int32, (ts, D_A), 1)
    low_head = (lane % LANES) < A_HEAD_DIM
    v_lo = jnp.where(low_head, v, 0.0).astype(_BF16)
    v_hi = jnp.where(low_head, 0.0, v).astype(_BF16)
    bias = abias_ref[...]
    ya_chunks = []
    for c in range(ts // CHUNK):
        rows = slice(c * CHUNK, (c + 1) * CHUNK)
        mix = []
        for p in range(HEAD_PAIRS):
            cols = slice(p * LANES, (p + 1) * LANES)
            rhs = jnp.concatenate([v_lo[rows, cols], v_hi[rows, cols]], axis=0)
            mix.append(jnp.dot(wp_scr[p], rhs, preferred_element_type=_F32))
        ya_chunks.append(u[rows] * (jnp.concatenate(mix, axis=1) + bias))
    ya = jnp.concatenate(ya_chunks, axis=0)

    glu = zb_a * jax.nn.sigmoid(zb_g)
    cx = zc_c * zc_x
    if sample:
        nseq = ts // SAMPLE_T
        sb_scr[:, 0:CONV_B - 1, :] = hb_ref[...]
        sb_scr[:, CONV_B - 1:CONV_B - 1 + SAMPLE_T, :] = glu.reshape(nseq, SAMPLE_T, D_B)
        sc_scr[:, 0:CONV_C - 1, :] = hc_ref[...]
        sc_scr[:, CONV_C - 1:CONV_C - 1 + SAMPLE_T, :] = cx.reshape(nseq, SAMPLE_T, D_C)
        acc = jnp.zeros((nseq, SAMPLE_T, D_B), _F32)
        for k in range(CONV_B):
            acc = acc + bcw_ref[pl.ds(k, 1), :] * sb_scr[:, k:k + SAMPLE_T, :]
        yb = acc.reshape(ts, D_B) + bcb_ref[...]
        accc = jnp.zeros((nseq, SAMPLE_T, D_C), _F32)
        for k in range(CONV_C):
            accc = accc + ccw_ref[pl.ds(k, 1), :] * sc_scr[:, k:k + SAMPLE_T, :]
        conv_c = accc.reshape(ts, D_C)
        newb_ref[...] = sb_scr[:, SAMPLE_T:SAMPLE_T + CONV_B - 1, :]
        newc_ref[...] = sc_scr[:, SAMPLE_T:SAMPLE_T + CONV_C - 1, :]
    else:
        @pl.when(seq_tile == 0)
        def _():
            sb_scr[0:HIST_B_PAD, :] = jnp.zeros((HIST_B_PAD, D_B), _F32)
            sc_scr[0:HIST_C_PAD, :] = jnp.zeros((HIST_C_PAD, D_C), _F32)

        sb_scr[HIST_B_PAD:HIST_B_PAD + ts, :] = glu
        sc_scr[HIST_C_PAD:HIST_C_PAD + ts, :] = cx
        off_b = HIST_B_PAD - (CONV_B - 1)
        off_c = HIST_C_PAD - (CONV_C - 1)
        yb_chunks = []
        cc_chunks = []
        for r0 in range(0, ts, CONV_ROWS):
            acc = None
            for phase in range(SUBLANES):
                taps = [k for k in range(CONV_B) if (k + off_b) % SUBLANES == phase]
                span = max(k + off_b for k in taps) - phase
                win = sb_scr[pl.ds(r0 + phase, CONV_ROWS + span), :]
                part = None
                for k in taps:
                    a = k + off_b - phase
                    term = bcw_ref[pl.ds(k, 1), :] * win[a:a + CONV_ROWS]
                    part = term if part is None else part + term
                acc = part if acc is None else acc + part
            yb_chunks.append(acc)
            accc = jnp.zeros((CONV_ROWS, D_C), _F32)
            for k in range(CONV_C):
                accc = accc + ccw_ref[pl.ds(k, 1), :] * sc_scr[pl.ds(r0 + k + off_c, CONV_ROWS), :]
            cc_chunks.append(accc)
        yb = jnp.concatenate(yb_chunks, axis=0) + bcb_ref[...]
        conv_c = jnp.concatenate(cc_chunks, axis=0)

        newb_ref[0] = sb_scr[pl.ds(HIST_B_PAD + ts - (CONV_B - 1), CONV_B - 1), :]
        newc_ref[0] = sc_scr[pl.ds(HIST_C_PAD + ts - (CONV_C - 1), CONV_C - 1), :]
        sb_scr[0:HIST_B_PAD, :] = sb_scr[ts:ts + HIST_B_PAD, :]
        sc_scr[0:HIST_C_PAD, :] = sc_scr[ts:ts + HIST_C_PAD, :]

    yb = jax.nn.silu(_layernorm(yb, blng_ref[...], blnb_ref[...]))
    yc = zc_b * conv_c

    ymix = jnp.concatenate([ya, yb, yc], axis=-1).astype(_BF16)
    y = jnp.dot(ymix, wout_ref[...], preferred_element_type=_F32)
    out_ref[...] = (x + y).reshape(out_ref.shape)


def _const_spec(shape):
    nd = len(shape)
    return pl.BlockSpec(shape, lambda *_: (0,) * nd, pipeline_mode=pl.Buffered(1))


def _mixer_weight_specs():
    return [
        _const_spec((1, D_MODEL)),
        _const_spec((D_MODEL, D_IN)),
        _const_spec((1, D_A)),
        _const_spec((1, D_A)),
        _const_spec((A_HEADS, CHUNK, CHUNK)),
        _const_spec((CHUNK, D_A)),
        _const_spec((CONV_B, D_B)),
        _const_spec((1, D_B)),
        _const_spec((1, D_B)),
        _const_spec((1, D_B)),
        _const_spec((CONV_C, D_C)),
        _const_spec((D_MODEL, D_MODEL)),
    ]


def _mixer_prompt(x, lw):
    batch, seq, _ = x.shape
    ts = ROW_TILE
    grid = (batch, seq // ts)
    return pl.pallas_call(
        functools.partial(_mixer_kernel, False, ts),
        grid=grid,
        in_specs=[pl.BlockSpec((1, ts, D_MODEL), lambda b, s: (b, s, 0))] + _mixer_weight_specs(),
        out_specs=[
            pl.BlockSpec((1, ts, D_MODEL), lambda b, s: (b, s, 0)),
            pl.BlockSpec((1, CONV_B - 1, D_B), lambda b, s: (b, 0, 0)),
            pl.BlockSpec((1, CONV_C - 1, D_C), lambda b, s: (b, 0, 0)),
        ],
        out_shape=[
            jax.ShapeDtypeStruct((batch, seq, D_MODEL), _F32),
            jax.ShapeDtypeStruct((batch, CONV_B - 1, D_B), _F32),
            jax.ShapeDtypeStruct((batch, CONV_C - 1, D_C), _F32),
        ],
        scratch_shapes=[
            pltpu.VMEM((HEAD_PAIRS, CHUNK, 2 * CHUNK), _BF16),
            pltpu.VMEM((HIST_B_PAD + ts, D_B), _F32),
            pltpu.VMEM((HIST_C_PAD + ts, D_C), _F32),
        ],
        compiler_params=pltpu.CompilerParams(
            dimension_semantics=("arbitrary", "arbitrary"),
            vmem_limit_bytes=VMEM_LIMIT_BYTES),
        name="mixer_prompt",
    )(x, *lw)


def _mixer_sample(x, lw, hist_b, hist_c):
    rows = x.shape[0]
    ts = ROW_TILE
    nseq = ts // SAMPLE_T
    grid = (rows // ts,)
    return pl.pallas_call(
        functools.partial(_mixer_kernel, True, ts),
        grid=grid,
        in_specs=[pl.BlockSpec((ts, D_MODEL), lambda i: (i, 0))] + _mixer_weight_specs() + [
            pl.BlockSpec((nseq, CONV_B - 1, D_B), lambda i: (i, 0, 0)),
            pl.BlockSpec((nseq, CONV_C - 1, D_C), lambda i: (i, 0, 0)),
        ],
        out_specs=[
            pl.BlockSpec((ts, D_MODEL), lambda i: (i, 0)),
            pl.BlockSpec((nseq, CONV_B - 1, D_B), lambda i: (i, 0, 0)),
            pl.BlockSpec((nseq, CONV_C - 1, D_C), lambda i: (i, 0, 0)),
            pl.BlockSpec((ts, D_A), lambda i: (i, 0)),
        ],
        out_shape=[
            jax.ShapeDtypeStruct((rows, D_MODEL), _F32),
            jax.ShapeDtypeStruct((rows // SAMPLE_T, CONV_B - 1, D_B), _F32),
            jax.ShapeDtypeStruct((rows // SAMPLE_T, CONV_C - 1, D_C), _F32),
            jax.ShapeDtypeStruct((rows, D_A), _F32),
        ],
        scratch_shapes=[
            pltpu.VMEM((HEAD_PAIRS, CHUNK, 2 * CHUNK), _BF16),
            pltpu.VMEM((nseq, HIST_B_PAD + SAMPLE_T, D_B), _F32),
            pltpu.VMEM((nseq, HIST_C_PAD + SAMPLE_T, D_C), _F32),
        ],
        compiler_params=pltpu.CompilerParams(
            dimension_semantics=("arbitrary",),
            vmem_limit_bytes=VMEM_LIMIT_BYTES),
        name="mixer_sample",
    )(x, *lw, hist_b, hist_c)


def _ffn_kernel(final, x_ref, g_ref, w1_ref, w2_ref, gf_ref, out_ref):
    x = x_ref[...]
    h = _rmsnorm(x, g_ref[...]).astype(_BF16)
    acc = x
    for j in range(D_FF // FF_CHUNK):
        cols = slice(j * FF_CHUNK, (j + 1) * FF_CHUNK)
        a = jnp.dot(h, w1_ref[:, cols], preferred_element_type=_F32)
        a = jnp.square(jnp.maximum(a, 0.0)).astype(_BF16)
        acc = acc + jnp.dot(a, w2_ref[cols, :], preferred_element_type=_F32)
    if final:
        acc = _rmsnorm(acc, gf_ref[...])
    out_ref[...] = acc


def _ffn(x, g, w1, w2, gf, final):
    rows = x.shape[0]
    tm = ROW_TILE
    return pl.pallas_call(
        functools.partial(_ffn_kernel, final),
        grid=(rows // tm,),
        in_specs=[
            pl.BlockSpec((tm, D_MODEL), lambda i: (i, 0)),
            _const_spec((1, D_MODEL)),
            _const_spec((D_MODEL, D_FF)),
            _const_spec((D_FF, D_MODEL)),
            _const_spec((1, D_MODEL)),
        ],
        out_specs=pl.BlockSpec((tm, D_MODEL), lambda i: (i, 0)),
        out_shape=jax.ShapeDtypeStruct((rows, D_MODEL), _F32),
        compiler_params=pltpu.CompilerParams(
            dimension_semantics=("arbitrary",),
            vmem_limit_bytes=VMEM_LIMIT_BYTES),
        name="ffn_final" if final else "ffn",
    )(x, g, w1, w2, gf)


def kernel(x_prompt, x_sample, state_conv_b, state_conv_c, norm1_g, w_in, a_ln_g, a_ln_b, a_ws,
           a_bias, b_conv_w, b_conv_b, b_ln_g, b_ln_b, c_conv_w, w_out, norm2_g, w_ff1, w_ff2,
           norm_f_g):
    depth = w_in.shape[0]
    batch, seq, _ = x_prompt.shape
    dec_batch, dec_seq, _ = x_sample.shape
    assert dec_seq == SAMPLE_T and seq % ROW_TILE == 0 and (dec_batch * dec_seq) % ROW_TILE == 0

    xp = x_prompt
    xs = x_sample.reshape(dec_batch * dec_seq, D_MODEL)
    gf = norm_f_g.reshape(1, D_MODEL)
    reps = CHUNK // SAMPLE_T
    cb_p, cc_p, cb_s, cc_s, v_s = [], [], [], [], []
    for l in range(depth):
        final = l == depth - 1
        common = (
            norm1_g[l].reshape(1, D_MODEL),
            w_in[l].astype(_BF16),
            a_ln_g[l].reshape(1, D_A),
            a_ln_b[l].reshape(1, D_A),
        )
        tail = (
            b_conv_w[l],
            b_conv_b[l].reshape(1, D_B),
            b_ln_g[l].reshape(1, D_B),
            b_ln_b[l].reshape(1, D_B),
            c_conv_w[l],
            w_out[l].astype(_BF16),
        )
        bias_p = jnp.repeat(a_bias[l].T, A_HEAD_DIM, axis=1)
        bias_s = jnp.tile(bias_p[:SAMPLE_T], (reps, 1))
        aw_p = a_ws[l]
        aw_s = jnp.tile(a_ws[l][:, :SAMPLE_T, :SAMPLE_T], (1, reps, reps))
        g2 = norm2_g[l].reshape(1, D_MODEL)
        w1 = w_ff1[l].astype(_BF16)
        w2 = w_ff2[l].astype(_BF16)

        xp, nbp, ncp = _mixer_prompt(xp, common + (aw_p, bias_p) + tail)
        xp = _ffn(xp.reshape(batch * seq, D_MODEL), g2, w1, w2, gf, final).reshape(batch, seq, D_MODEL)
        xs, nbs, ncs, vs = _mixer_sample(xs, common + (aw_s, bias_s) + tail,
                                         state_conv_b[l], state_conv_c[l])
        xs = _ffn(xs, g2, w1, w2, gf, final)
        cb_p.append(nbp); cc_p.append(ncp); cb_s.append(nbs); cc_s.append(ncs)
        v_s.append(vs.reshape(dec_batch, dec_seq, D_A))

    return (xp, xs.reshape(dec_batch, dec_seq, D_MODEL), jnp.stack(cb_p), jnp.stack(cc_p),
            jnp.stack(cb_s), jnp.stack(cc_s), jnp.stack(v_s))
```

```python
import functools

import jax
import jax.numpy as jnp
from jax import lax
from jax.experimental import pallas as pl
from jax.experimental.pallas import tpu as pltpu

D_MODEL = 1024
D_A = 512
A_HEADS = 8
A_HEAD_DIM = D_A // A_HEADS
CHUNK = 128
D_B = 256
CONV_B = 31
D_C = 256
CONV_C = 3
D_IN = 2 * D_A + 2 * D_B + 3 * D_C
D_FF = 4 * D_MODEL
EPS = 1e-6

SUBLANES = 8
LANES = 128
HEAD_PAIRS = D_A // LANES
VMEM_LIMIT_BYTES = 56 * 1024 * 1024

ROW_TILE = 512
FF_CHUNK = 1024
CONV_ROWS = 64
ROW_PITCH = 2
HIST_B_PAD = 32
HIST_C_PAD = 8
SAMPLE_T = 8

_F32 = jnp.float32
_BF16 = jnp.bfloat16


def _rmsnorm(x, g):
    return x * lax.rsqrt(jnp.mean(x * x, axis=-1, keepdims=True) + EPS) * g


def _layernorm(x, g, b):
    mu = jnp.mean(x, axis=-1, keepdims=True)
    xc = x - mu
    var = jnp.mean(xc * xc, axis=-1, keepdims=True)
    return xc * lax.rsqrt(var + EPS) * g + b


def _stage_in(x, n1g_ref, win_ref):
    h = _rmsnorm(x, n1g_ref[...]).astype(_BF16)
    return jnp.dot(h, win_ref[...], preferred_element_type=_F32)


def _pitched_rows(start, size):
    return pl.ds(ROW_PITCH * start, size, stride=ROW_PITCH)


def _causal_conv(src_scr, w_ref, ntaps, hist_pad, ts):
    off = hist_pad - (ntaps - 1)
    outs = []
    for s in range(src_scr.shape[0]):
        lanes = slice(s * LANES, (s + 1) * LANES)
        chunks = []
        for r0 in range(0, ts, CONV_ROWS):
            acc = None
            for phase in range(min(SUBLANES, ntaps)):
                taps = range(phase, ntaps, SUBLANES)
                win = src_scr[s, _pitched_rows(r0 + off + phase, CONV_ROWS + taps[-1] - phase), :]
                for k in taps:
                    term = w_ref[pl.ds(k, 1), lanes] * win[k - phase:k - phase + CONV_ROWS]
                    acc = term if acc is None else acc + term
            chunks.append(acc)
        outs.append(jnp.concatenate(chunks, axis=0))
    return jnp.concatenate(outs, axis=1)


def _stage_mid(sample, ts, zcols, w, wp_scr, sb_scr, sc_scr, hist_b, hist_c):
    (alng_ref, alnb_ref, abias_ref, bcw_ref, bcb_ref, blng_ref, blnb_ref, ccw_ref) = w
    o = 0
    za_u = zcols(o, o + D_A); o += D_A
    za_v = zcols(o, o + D_A); o += D_A
    zb_a = zcols(o, o + D_B); o += D_B
    zb_g = zcols(o, o + D_B); o += D_B
    zc_b = zcols(o, o + D_C); o += D_C
    zc_c = zcols(o, o + D_C); o += D_C
    zc_x = zcols(o, o + D_C)

    u = jax.nn.gelu(za_u)
    v = _layernorm(jax.nn.gelu(za_v), alng_ref[...], alnb_ref[...])
    lane = lax.broadcasted_iota(jnp.int32, (ts, D_A), 1)
    low_head = (lane % LANES) < A_HEAD_DIM
    v_lo = jnp.where(low_head, v, 0.0).astype(_BF16)
    v_hi = jnp.where(low_head, 0.0, v).astype(_BF16)
    bias = abias_ref[...]
    ya_chunks = []
    for c in range(ts // CHUNK):
        rows = slice(c * CHUNK, (c + 1) * CHUNK)
        mix = []
        for p in range(HEAD_PAIRS):
            cols = slice(p * LANES, (p + 1) * LANES)
            rhs = jnp.concatenate([v_lo[rows, cols], v_hi[rows, cols]], axis=0)
            mix.append(jnp.dot(wp_scr[p], rhs, preferred_element_type=_F32))
        ya_chunks.append(u[rows] * (jnp.concatenate(mix, axis=1) + bias))
    ya = jnp.concatenate(ya_chunks, axis=0)

    glu = zb_a * jax.nn.sigmoid(zb_g)
    cx = zc_c * zc_x
    if sample:
        nseq = ts // SAMPLE_T
        sb_scr[:, 0:CONV_B - 1, :] = hist_b
        sb_scr[:, CONV_B - 1:CONV_B - 1 + SAMPLE_T, :] = glu.reshape(nseq, SAMPLE_T, D_B)
        sc_scr[:, 0:CONV_C - 1, :] = hist_c
        sc_scr[:, CONV_C - 1:CONV_C - 1 + SAMPLE_T, :] = cx.reshape(nseq, SAMPLE_T, D_C)
        acc = jnp.zeros((nseq, SAMPLE_T, D_B), _F32)
        for k in range(CONV_B):
            acc = acc + bcw_ref[pl.ds(k, 1), :] * sb_scr[:, k:k + SAMPLE_T, :]
        yb = acc.reshape(ts, D_B) + bcb_ref[...]
        accc = jnp.zeros((nseq, SAMPLE_T, D_C), _F32)
        for k in range(CONV_C):
            accc = accc + ccw_ref[pl.ds(k, 1), :] * sc_scr[:, k:k + SAMPLE_T, :]
        conv_c = accc.reshape(ts, D_C)
        new_b = sb_scr[:, SAMPLE_T:SAMPLE_T + CONV_B - 1, :]
        new_c = sc_scr[:, SAMPLE_T:SAMPLE_T + CONV_C - 1, :]
    else:
        for s in range(D_B // LANES):
            sb_scr[s, _pitched_rows(HIST_B_PAD, ts), :] = glu[:, s * LANES:(s + 1) * LANES]
        for s in range(D_C // LANES):
            sc_scr[s, _pitched_rows(HIST_C_PAD, ts), :] = cx[:, s * LANES:(s + 1) * LANES]
        yb = _causal_conv(sb_scr, bcw_ref, CONV_B, HIST_B_PAD, ts) + bcb_ref[...]
        conv_c = _causal_conv(sc_scr, ccw_ref, CONV_C, HIST_C_PAD, ts)

        new_b = jnp.concatenate(
            [sb_scr[s, _pitched_rows(HIST_B_PAD + ts - (CONV_B - 1), CONV_B - 1), :]
             for s in range(D_B // LANES)], axis=1)
        new_c = jnp.concatenate(
            [sc_scr[s, _pitched_rows(HIST_C_PAD + ts - (CONV_C - 1), CONV_C - 1), :]
             for s in range(D_C // LANES)], axis=1)
        sb_scr[:, 0:ROW_PITCH * HIST_B_PAD, :] = sb_scr[:, ROW_PITCH * ts:ROW_PITCH * (ts + HIST_B_PAD), :]
        sc_scr[:, 0:ROW_PITCH * HIST_C_PAD, :] = sc_scr[:, ROW_PITCH * ts:ROW_PITCH * (ts + HIST_C_PAD), :]

    yb = jax.nn.silu(_layernorm(yb, blng_ref[...], blnb_ref[...]))
    yc = zc_b * conv_c
    return ya, yb, yc, new_b, new_c, v


def _stage_out(ya, yb, yc, x, wout_ref):
    y = jnp.dot(ya.astype(_BF16), wout_ref[0:D_A, :], preferred_element_type=_F32)
    y = y + jnp.dot(yc.astype(_BF16), wout_ref[D_A + D_B:D_MODEL, :], preferred_element_type=_F32)
    y = y + jnp.dot(yb.astype(_BF16), wout_ref[D_A:D_A + D_B, :], preferred_element_type=_F32)
    return x + y


def _build_position_weights(sample, aw_ref, wp_scr):
    row = lax.broadcasted_iota(jnp.int32, (CHUNK, CHUNK), 0)
    col = lax.broadcasted_iota(jnp.int32, (CHUNK, CHUNK), 1)
    keep = row >= col
    if sample:
        keep = jnp.logical_and(keep, (row // SAMPLE_T) == (col // SAMPLE_T))
    for p in range(HEAD_PAIRS):
        w0 = jnp.where(keep, aw_ref[2 * p], 0.0)
        w1 = jnp.where(keep, aw_ref[2 * p + 1], 0.0)
        wp_scr[p] = jnp.concatenate([w0, w1], axis=1).astype(_BF16)


def _mixer_prompt_kernel(ts, tiles_per_seq,
                         x_odd_ref, x_next_ref, x_even_ref, n1g_ref, win_ref, alng_ref, alnb_ref,
                         aw_ref, abias_ref, bcw_ref, bcb_ref, blng_ref, blnb_ref, ccw_ref, wout_ref,
                         out_ref, newb_ref, newc_ref,
                         wp_scr, z_even_scr, z_odd_scr, sb_scr, sc_scr):
    j = pl.program_id(0)
    w_mid = (alng_ref, alnb_ref, abias_ref, bcw_ref, bcb_ref, blng_ref, blnb_ref, ccw_ref)

    @pl.when(j == 0)
    def _():
        _build_position_weights(False, aw_ref, wp_scr)
        z_even_scr[...] = _stage_in(x_even_ref[...], n1g_ref, win_ref)
        sb_scr[...] = jnp.zeros(sb_scr.shape, _F32)
        sc_scr[...] = jnp.zeros(sc_scr.shape, _F32)

    @pl.when((2 * j) % tiles_per_seq == 0)
    def _():
        sb_scr[:, 0:ROW_PITCH * HIST_B_PAD, :] = jnp.zeros(
            (D_B // LANES, ROW_PITCH * HIST_B_PAD, LANES), _F32)
        sc_scr[:, 0:ROW_PITCH * HIST_C_PAD, :] = jnp.zeros(
            (D_C // LANES, ROW_PITCH * HIST_C_PAD, LANES), _F32)

    z_odd_scr[...] = _stage_in(x_odd_ref[...], n1g_ref, win_ref)
    ya, yb, yc, _, _, _ = _stage_mid(False, ts, lambda a, b: z_even_scr[:, a:b], w_mid,
                                     wp_scr, sb_scr, sc_scr, None, None)
    out_ref[0:ts, :] = _stage_out(ya, yb, yc, x_even_ref[...], wout_ref)

    z_even_scr[...] = _stage_in(x_next_ref[...], n1g_ref, win_ref)
    ya, yb, yc, new_b, new_c, _ = _stage_mid(False, ts, lambda a, b: z_odd_scr[:, a:b], w_mid,
                                             wp_scr, sb_scr, sc_scr, None, None)
    out_ref[ts:2 * ts, :] = _stage_out(ya, yb, yc, x_odd_ref[...], wout_ref)
    newb_ref[0] = new_b
    newc_ref[0] = new_c


def _mixer_sample_kernel(nsub, ts,
                         x_ref, n1g_ref, win_ref, alng_ref, alnb_ref, aw_ref, abias_ref, bcw_ref,
                         bcb_ref, blng_ref, blnb_ref, ccw_ref, wout_ref, hb_ref, hc_ref,
                         out_ref, newb_ref, newc_ref, v_ref, wp_scr, sb_scr, sc_scr):
    w_mid = (alng_ref, alnb_ref, abias_ref, bcw_ref, bcb_ref, blng_ref, blnb_ref, ccw_ref)
    _build_position_weights(True, aw_ref, wp_scr)
    nseq = ts // SAMPLE_T
    zs = [_stage_in(x_ref[j], n1g_ref, win_ref) for j in range(nsub)]
    for j in range(nsub):
        seqs = slice(j * nseq, (j + 1) * nseq)
        z = zs[j]
        ya, yb, yc, new_b, new_c, v = _stage_mid(True, ts, lambda a, b: z[:, a:b], w_mid, wp_scr,
                                                 sb_scr.at[j], sc_scr.at[j], hb_ref[seqs], hc_ref[seqs])
        out_ref[j] = _stage_out(ya, yb, yc, x_ref[j], wout_ref)
        newb_ref[seqs] = new_b
        newc_ref[seqs] = new_c
        v_ref[j] = v


def _const_spec(shape):
    nd = len(shape)
    return pl.BlockSpec(shape, lambda *_: (0,) * nd, pipeline_mode=pl.Buffered(1))


def _mixer_weight_specs():
    return [
        _const_spec((1, D_MODEL)),
        _const_spec((D_MODEL, D_IN)),
        _const_spec((1, D_A)),
        _const_spec((1, D_A)),
        _const_spec((A_HEADS, CHUNK, CHUNK)),
        _const_spec((CHUNK, D_A)),
        _const_spec((CONV_B, D_B)),
        _const_spec((1, D_B)),
        _const_spec((1, D_B)),
        _const_spec((1, D_B)),
        _const_spec((CONV_C, D_C)),
        _const_spec((D_MODEL, D_MODEL)),
    ]


def _mixer_prompt(x, lw, seq):
    rows = x.shape[0]
    ts = ROW_TILE
    n_tiles = rows // ts
    tiles_per_seq = seq // ts
    assert tiles_per_seq % 2 == 0
    batch = rows // seq

    def tile_spec(index):
        return pl.BlockSpec((ts, D_MODEL), index)

    return pl.pallas_call(
        functools.partial(_mixer_prompt_kernel, ts, tiles_per_seq),
        grid=(n_tiles // 2,),
        in_specs=[
            tile_spec(lambda j: (2 * j + 1, 0)),
            tile_spec(lambda j: (jnp.minimum(2 * j + 2, n_tiles - 1), 0)),
            tile_spec(lambda j: (2 * j, 0)),
        ] + _mixer_weight_specs(),
        out_specs=[
            pl.BlockSpec((2 * ts, D_MODEL), lambda j: (j, 0)),
            pl.BlockSpec((1, CONV_B - 1, D_B), lambda j: ((2 * j + 1) // tiles_per_seq, 0, 0)),
            pl.BlockSpec((1, CONV_C - 1, D_C), lambda j: ((2 * j + 1) // tiles_per_seq, 0, 0)),
        ],
        out_shape=[
            jax.ShapeDtypeStruct((rows, D_MODEL), _F32),
            jax.ShapeDtypeStruct((batch, CONV_B - 1, D_B), _F32),
            jax.ShapeDtypeStruct((batch, CONV_C - 1, D_C), _F32),
        ],
        scratch_shapes=[
            pltpu.VMEM((HEAD_PAIRS, CHUNK, 2 * CHUNK), _BF16),
            pltpu.VMEM((ts, D_IN), _F32),
            pltpu.VMEM((ts, D_IN), _F32),
            pltpu.VMEM((D_B // LANES, ROW_PITCH * (HIST_B_PAD + ts), LANES), _F32),
            pltpu.VMEM((D_C // LANES, ROW_PITCH * (HIST_C_PAD + ts), LANES), _F32),
        ],
        compiler_params=pltpu.CompilerParams(
            dimension_semantics=("arbitrary",),
            vmem_limit_bytes=VMEM_LIMIT_BYTES),
        name="mixer_prompt",
    )(x, x, x, *lw)


def _mixer_sample(x, lw, hist_b, hist_c):
    nsub, ts, _ = x.shape
    nseq = ts // SAMPLE_T

    def full(shape):
        return pl.BlockSpec(shape, lambda i: (0,) * len(shape))

    return pl.pallas_call(
        functools.partial(_mixer_sample_kernel, nsub, ts),
        grid=(1,),
        in_specs=[full((nsub, ts, D_MODEL))] + _mixer_weight_specs() + [
            full((nsub * nseq, CONV_B - 1, D_B)),
            full((nsub * nseq, CONV_C - 1, D_C)),
        ],
        out_specs=[
            full((nsub, ts, D_MODEL)),
            full((nsub * nseq, CONV_B - 1, D_B)),
            full((nsub * nseq, CONV_C - 1, D_C)),
            full((nsub, ts, D_A)),
        ],
        out_shape=[
            jax.ShapeDtypeStruct((nsub, ts, D_MODEL), _F32),
            jax.ShapeDtypeStruct((nsub * nseq, CONV_B - 1, D_B), _F32),
            jax.ShapeDtypeStruct((nsub * nseq, CONV_C - 1, D_C), _F32),
            jax.ShapeDtypeStruct((nsub, ts, D_A), _F32),
        ],
        scratch_shapes=[
            pltpu.VMEM((HEAD_PAIRS, CHUNK, 2 * CHUNK), _BF16),
            pltpu.VMEM((nsub, nseq, HIST_B_PAD + SAMPLE_T, D_B), _F32),
            pltpu.VMEM((nsub, nseq, HIST_C_PAD + SAMPLE_T, D_C), _F32),
        ],
        compiler_params=pltpu.CompilerParams(
            dimension_semantics=("arbitrary",),
            vmem_limit_bytes=VMEM_LIMIT_BYTES),
        name="mixer_sample",
    )(x, *lw, hist_b, hist_c)


def _ffn_kernel(final, x_ref, g_ref, w1_ref, w2_ref, gf_ref, out_ref):
    x = x_ref[...]
    h = _rmsnorm(x, g_ref[...]).astype(_BF16)
    acc = x
    for j in range(D_FF // FF_CHUNK):
        cols = slice(j * FF_CHUNK, (j + 1) * FF_CHUNK)
        a = jnp.dot(h, w1_ref[:, cols], preferred_element_type=_F32)
        a = jnp.square(jnp.maximum(a, 0.0)).astype(_BF16)
        acc = acc + jnp.dot(a, w2_ref[cols, :], preferred_element_type=_F32)
    if final:
        acc = _rmsnorm(acc, gf_ref[...])
    out_ref[...] = acc


def _ffn(x, g, w1, w2, gf, final):
    rows = x.shape[0]
    tm = ROW_TILE
    return pl.pallas_call(
        functools.partial(_ffn_kernel, final),
        grid=(rows // tm,),
        in_specs=[
            pl.BlockSpec((tm, D_MODEL), lambda i: (i, 0)),
            _const_spec((1, D_MODEL)),
            _const_spec((D_MODEL, D_FF)),
            _const_spec((D_FF, D_MODEL)),
            _const_spec((1, D_MODEL)),
        ],
        out_specs=pl.BlockSpec((tm, D_MODEL), lambda i: (i, 0)),
        out_shape=jax.ShapeDtypeStruct((rows, D_MODEL), _F32),
        compiler_params=pltpu.CompilerParams(
            dimension_semantics=("arbitrary",),
            vmem_limit_bytes=VMEM_LIMIT_BYTES),
        name="ffn_final" if final else "ffn",
    )(x, g, w1, w2, gf)


def kernel(x_prompt, x_sample, state_conv_b, state_conv_c, norm1_g, w_in, a_ln_g, a_ln_b, a_ws,
           a_bias, b_conv_w, b_conv_b, b_ln_g, b_ln_b, c_conv_w, w_out, norm2_g, w_ff1, w_ff2,
           norm_f_g):
    depth = w_in.shape[0]
    batch, seq, _ = x_prompt.shape
    dec_batch, dec_seq, _ = x_sample.shape
    dec_rows = dec_batch * dec_seq
    assert dec_seq == SAMPLE_T and seq % ROW_TILE == 0 and dec_rows % ROW_TILE == 0

    xp = x_prompt.reshape(batch * seq, D_MODEL)
    xs = x_sample.reshape(dec_rows, D_MODEL)
    gf = norm_f_g.reshape(1, D_MODEL)
    reps = CHUNK // SAMPLE_T
    cb_p, cc_p, cb_s, cc_s, v_s = [], [], [], [], []
    for l in range(depth):
        final = l == depth - 1
        common = (
            norm1_g[l].reshape(1, D_MODEL),
            w_in[l].astype(_BF16),
            a_ln_g[l].reshape(1, D_A),
            a_ln_b[l].reshape(1, D_A),
        )
        tail = (
            b_conv_w[l],
            b_conv_b[l].reshape(1, D_B),
            b_ln_g[l].reshape(1, D_B),
            b_ln_b[l].reshape(1, D_B),
            c_conv_w[l],
            w_out[l].astype(_BF16),
        )
        bias_p = jnp.repeat(a_bias[l].T, A_HEAD_DIM, axis=1)
        bias_s = jnp.tile(bias_p[:SAMPLE_T], (reps, 1))
        aw_p = a_ws[l]
        aw_s = jnp.tile(a_ws[l][:, :SAMPLE_T, :SAMPLE_T], (1, reps, reps))
        g2 = norm2_g[l].reshape(1, D_MODEL)
        w1 = w_ff1[l].astype(_BF16)
        w2 = w_ff2[l].astype(_BF16)

        xp, nbp, ncp = _mixer_prompt(xp, common + (aw_p, bias_p) + tail, seq)
        xp = _ffn(xp, g2, w1, w2, gf, final)
        xs, nbs, ncs, vs = _mixer_sample(xs.reshape(dec_rows // ROW_TILE, ROW_TILE, D_MODEL),
                                         common + (aw_s, bias_s) + tail,
                                         state_conv_b[l], state_conv_c[l])
        xs = _ffn(xs.reshape(dec_rows, D_MODEL), g2, w1, w2, gf, final)
        cb_p.append(nbp); cc_p.append(ncp); cb_s.append(nbs); cc_s.append(ncs)
        v_s.append(vs.reshape(dec_batch, dec_seq, D_A))

    return (xp.reshape(batch, seq, D_MODEL), xs.reshape(dec_batch, dec_seq, D_MODEL),
            jnp.stack(cb_p), jnp.stack(cc_p), jnp.stack(cb_s), jnp.stack(cc_s), jnp.stack(v_s))
```

```python
import functools

import jax
import jax.numpy as jnp
from jax import lax
from jax.experimental import pallas as pl
from jax.experimental.pallas import tpu as pltpu

D_MODEL = 1024
D_A = 512
A_HEADS = 8
A_HEAD_DIM = D_A // A_HEADS
CHUNK = 128
D_B = 256
CONV_B = 31
D_C = 256
CONV_C = 3
D_IN = 2 * D_A + 2 * D_B + 3 * D_C
D_FF = 4 * D_MODEL
EPS = 1e-6

SUBLANES = 8
LANES = 128
HEAD_PAIRS = D_A // LANES
VMEM_LIMIT_BYTES = 56 * 1024 * 1024

ROW_TILE = 512
FF_CHUNK = 1024
CONV_ROWS = 64
ROW_PITCH = 2
HIST_B_PAD = 32
HIST_C_PAD = 8
SAMPLE_T = 8

_F32 = jnp.float32
_BF16 = jnp.bfloat16


def _rmsnorm(x, g):
    return x * lax.rsqrt(jnp.mean(x * x, axis=-1, keepdims=True) + EPS) * g


def _layernorm(x, g, b):
    mu = jnp.mean(x, axis=-1, keepdims=True)
    xc = x - mu
    var = jnp.mean(xc * xc, axis=-1, keepdims=True)
    return xc * lax.rsqrt(var + EPS) * g + b


def _stage_in(x, n1g_ref, win_ref):
    h = _rmsnorm(x, n1g_ref[...]).astype(_BF16)
    return jnp.dot(h, win_ref[...], preferred_element_type=_F32)


def _split_projection(zcols):
    parts = []
    o = 0
    for width in (D_A, D_A, D_B, D_B, D_C, D_C, D_C):
        parts.append(zcols(o, o + width))
        o += width
    return parts


def _stage_out(ya, yb, yc, x, wout_ref):
    y = jnp.dot(ya.astype(_BF16), wout_ref[0:D_A, :], preferred_element_type=_F32)
    y = y + jnp.dot(yc.astype(_BF16), wout_ref[D_A + D_B:D_MODEL, :], preferred_element_type=_F32)
    y = y + jnp.dot(yb.astype(_BF16), wout_ref[D_A:D_A + D_B, :], preferred_element_type=_F32)
    return x + y


def _pitched_rows(start, size):
    return pl.ds(ROW_PITCH * start, size, stride=ROW_PITCH)


def _causal_conv(src_scr, w_ref, ntaps, hist_pad, ts):
    off = hist_pad - (ntaps - 1)
    outs = []
    for s in range(src_scr.shape[0]):
        lanes = slice(s * LANES, (s + 1) * LANES)
        chunks = []
        for r0 in range(0, ts, CONV_ROWS):
            acc = None
            for phase in range(min(SUBLANES, ntaps)):
                taps = range(phase, ntaps, SUBLANES)
                win = src_scr[s, _pitched_rows(r0 + off + phase, CONV_ROWS + taps[-1] - phase), :]
                for k in taps:
                    term = w_ref[pl.ds(k, 1), lanes] * win[k - phase:k - phase + CONV_ROWS]
                    acc = term if acc is None else acc + term
            chunks.append(acc)
        outs.append(jnp.concatenate(chunks, axis=0))
    return jnp.concatenate(outs, axis=1)


def _prompt_mixers(ts, zcols, w, wp_scr, sb_scr, sc_scr):
    (alng_ref, alnb_ref, abias_ref, bcw_ref, bcb_ref, blng_ref, blnb_ref, ccw_ref) = w
    za_u, za_v, zb_a, zb_g, zc_b, zc_c, zc_x = _split_projection(zcols)

    u = jax.nn.gelu(za_u)
    v = _layernorm(jax.nn.gelu(za_v), alng_ref[...], alnb_ref[...])
    lane = lax.broadcasted_iota(jnp.int32, (ts, D_A), 1)
    low_head = (lane % LANES) < A_HEAD_DIM
    v_lo = jnp.where(low_head, v, 0.0).astype(_BF16)
    v_hi = jnp.where(low_head, 0.0, v).astype(_BF16)
    bias = abias_ref[...]
    ya_chunks = []
    for c in range(ts // CHUNK):
        rows = slice(c * CHUNK, (c + 1) * CHUNK)
        mix = []
        for p in range(HEAD_PAIRS):
            cols = slice(p * LANES, (p + 1) * LANES)
            rhs = jnp.concatenate([v_lo[rows, cols], v_hi[rows, cols]], axis=0)
            mix.append(jnp.dot(wp_scr[p], rhs, preferred_element_type=_F32))
        ya_chunks.append(u[rows] * (jnp.concatenate(mix, axis=1) + bias))
    ya = jnp.concatenate(ya_chunks, axis=0)

    glu = zb_a * jax.nn.sigmoid(zb_g)
    cx = zc_c * zc_x
    for s in range(D_B // LANES):
        sb_scr[s, _pitched_rows(HIST_B_PAD, ts), :] = glu[:, s * LANES:(s + 1) * LANES]
    for s in range(D_C // LANES):
        sc_scr[s, _pitched_rows(HIST_C_PAD, ts), :] = cx[:, s * LANES:(s + 1) * LANES]
    yb = _causal_conv(sb_scr, bcw_ref, CONV_B, HIST_B_PAD, ts) + bcb_ref[...]
    conv_c = _causal_conv(sc_scr, ccw_ref, CONV_C, HIST_C_PAD, ts)

    new_b = jnp.concatenate(
        [sb_scr[s, _pitched_rows(HIST_B_PAD + ts - (CONV_B - 1), CONV_B - 1), :]
         for s in range(D_B // LANES)], axis=1)
    new_c = jnp.concatenate(
        [sc_scr[s, _pitched_rows(HIST_C_PAD + ts - (CONV_C - 1), CONV_C - 1), :]
         for s in range(D_C // LANES)], axis=1)
    sb_scr[:, 0:ROW_PITCH * HIST_B_PAD, :] = sb_scr[:, ROW_PITCH * ts:ROW_PITCH * (ts + HIST_B_PAD), :]
    sc_scr[:, 0:ROW_PITCH * HIST_C_PAD, :] = sc_scr[:, ROW_PITCH * ts:ROW_PITCH * (ts + HIST_C_PAD), :]

    yb = jax.nn.silu(_layernorm(yb, blng_ref[...], blnb_ref[...]))
    yc = zc_b * conv_c
    return ya, yb, yc, new_b, new_c


def _build_position_weights(aw_ref, wp_scr):
    row = lax.broadcasted_iota(jnp.int32, (CHUNK, CHUNK), 0)
    col = lax.broadcasted_iota(jnp.int32, (CHUNK, CHUNK), 1)
    keep = row >= col
    for p in range(HEAD_PAIRS):
        w0 = jnp.where(keep, aw_ref[2 * p], 0.0)
        w1 = jnp.where(keep, aw_ref[2 * p + 1], 0.0)
        wp_scr[p] = jnp.concatenate([w0, w1], axis=1).astype(_BF16)


def _mixer_prompt_kernel(ts, tiles_per_seq,
                         x_odd_ref, x_next_ref, x_even_ref, n1g_ref, win_ref, alng_ref, alnb_ref,
                         aw_ref, abias_ref, bcw_ref, bcb_ref, blng_ref, blnb_ref, ccw_ref, wout_ref,
                         out_ref, newb_ref, newc_ref,
                         wp_scr, z_even_scr, z_odd_scr, sb_scr, sc_scr):
    j = pl.program_id(0)
    w_mid = (alng_ref, alnb_ref, abias_ref, bcw_ref, bcb_ref, blng_ref, blnb_ref, ccw_ref)

    @pl.when(j == 0)
    def _():
        _build_position_weights(aw_ref, wp_scr)
        z_even_scr[...] = _stage_in(x_even_ref[...], n1g_ref, win_ref)
        sb_scr[...] = jnp.zeros(sb_scr.shape, _F32)
        sc_scr[...] = jnp.zeros(sc_scr.shape, _F32)

    @pl.when((2 * j) % tiles_per_seq == 0)
    def _():
        sb_scr[:, 0:ROW_PITCH * HIST_B_PAD, :] = jnp.zeros(
            (D_B // LANES, ROW_PITCH * HIST_B_PAD, LANES), _F32)
        sc_scr[:, 0:ROW_PITCH * HIST_C_PAD, :] = jnp.zeros(
            (D_C // LANES, ROW_PITCH * HIST_C_PAD, LANES), _F32)

    z_odd_scr[...] = _stage_in(x_odd_ref[...], n1g_ref, win_ref)
    ya, yb, yc, _, _ = _prompt_mixers(ts, lambda a, b: z_even_scr[:, a:b], w_mid,
                                      wp_scr, sb_scr, sc_scr)
    out_ref[0:ts, :] = _stage_out(ya, yb, yc, x_even_ref[...], wout_ref)

    z_even_scr[...] = _stage_in(x_next_ref[...], n1g_ref, win_ref)
    ya, yb, yc, new_b, new_c = _prompt_mixers(ts, lambda a, b: z_odd_scr[:, a:b], w_mid,
                                              wp_scr, sb_scr, sc_scr)
    out_ref[ts:2 * ts, :] = _stage_out(ya, yb, yc, x_odd_ref[...], wout_ref)
    newb_ref[0] = new_b
    newc_ref[0] = new_c


def _layer_spec(layer, shape):
    nd = len(shape)
    return pl.BlockSpec((None,) + tuple(shape), lambda *_: (layer,) + (0,) * nd,
                        pipeline_mode=pl.Buffered(1))


def _mixer_prompt(x, p, layer, seq):
    rows = x.shape[0]
    ts = ROW_TILE
    n_tiles = rows // ts
    tiles_per_seq = seq // ts
    assert tiles_per_seq % 2 == 0
    batch = rows // seq
    spec = functools.partial(_layer_spec, layer)

    def tile_spec(index):
        return pl.BlockSpec((ts, D_MODEL), index)

    return pl.pallas_call(
        functools.partial(_mixer_prompt_kernel, ts, tiles_per_seq),
        grid=(n_tiles // 2,),
        in_specs=[
            tile_spec(lambda j: (2 * j + 1, 0)),
            tile_spec(lambda j: (jnp.minimum(2 * j + 2, n_tiles - 1), 0)),
            tile_spec(lambda j: (2 * j, 0)),
            spec((1, D_MODEL)),
            spec((D_MODEL, D_IN)),
            spec((1, D_A)),
            spec((1, D_A)),
            spec((A_HEADS, CHUNK, CHUNK)),
            spec((CHUNK, D_A)),
            spec((CONV_B, D_B)),
            spec((1, D_B)),
            spec((1, D_B)),
            spec((1, D_B)),
            spec((CONV_C, D_C)),
            spec((D_MODEL, D_MODEL)),
        ],
        out_specs=[
            pl.BlockSpec((2 * ts, D_MODEL), lambda j: (j, 0)),
            pl.BlockSpec((1, CONV_B - 1, D_B), lambda j: ((2 * j + 1) // tiles_per_seq, 0, 0)),
            pl.BlockSpec((1, CONV_C - 1, D_C), lambda j: ((2 * j + 1) // tiles_per_seq, 0, 0)),
        ],
        out_shape=[
            jax.ShapeDtypeStruct((rows, D_MODEL), _F32),
            jax.ShapeDtypeStruct((batch, CONV_B - 1, D_B), _F32),
            jax.ShapeDtypeStruct((batch, CONV_C - 1, D_C), _F32),
        ],
        scratch_shapes=[
            pltpu.VMEM((HEAD_PAIRS, CHUNK, 2 * CHUNK), _BF16),
            pltpu.VMEM((ts, D_IN), _F32),
            pltpu.VMEM((ts, D_IN), _F32),
            pltpu.VMEM((D_B // LANES, ROW_PITCH * (HIST_B_PAD + ts), LANES), _F32),
            pltpu.VMEM((D_C // LANES, ROW_PITCH * (HIST_C_PAD + ts), LANES), _F32),
        ],
        compiler_params=pltpu.CompilerParams(
            dimension_semantics=("arbitrary",),
            vmem_limit_bytes=VMEM_LIMIT_BYTES),
        name="mixer_prompt",
    )(x, x, x, p["n1g"], p["win"], p["alng"], p["alnb"], p["aw"], p["abias"], p["bcw"], p["bcb"],
      p["blng"], p["blnb"], p["ccw"], p["wout"])


def _mixer_sample_kernel(nseq,
                         x_ref, n1g_ref, win_ref, alng_ref, alnb_ref, coef_ref, abias_ref, bcw_ref,
                         bcb_ref, blng_ref, blnb_ref, ccw_ref, wout_ref, hb_ref, hc_ref,
                         out_ref, newb_ref, newc_ref, v_ref):
    x = x_ref[...]
    z = _stage_in(x, n1g_ref, win_ref)
    za_u, za_v, zb_a, zb_g, zc_b, zc_c, zc_x = _split_projection(lambda a, b: z[:, a:b])

    def token(arr, t):
        return arr[t * nseq:(t + 1) * nseq]

    u = jax.nn.gelu(za_u)
    v = _layernorm(jax.nn.gelu(za_v), alng_ref[...], alnb_ref[...])
    v_ref[...] = v
    ya_tokens = []
    for t in range(SAMPLE_T):
        mix = None
        for s in range(t + 1):
            term = coef_ref[t, pl.ds(s, 1), :] * token(v, s)
            mix = term if mix is None else mix + term
        ya_tokens.append(token(u, t) * (mix + abias_ref[pl.ds(t, 1), :]))
    ya = jnp.concatenate(ya_tokens, axis=0)

    glu = zb_a * jax.nn.sigmoid(zb_g)
    cx = zc_c * zc_x

    def conv(hist_ref, new_rows, w_ref, ntaps):
        hist = ntaps - 1
        outs = []
        for t in range(SAMPLE_T):
            acc = None
            for k in range(ntaps):
                r = t + k
                src = hist_ref[r] if r < hist else token(new_rows, r - hist)
                term = w_ref[pl.ds(k, 1), :] * src
                acc = term if acc is None else acc + term
            outs.append(acc)
        return jnp.concatenate(outs, axis=0)

    yb = conv(hb_ref, glu, bcw_ref, CONV_B) + bcb_ref[...]
    conv_c = conv(hc_ref, cx, ccw_ref, CONV_C)
    keep_b = CONV_B - 1 - SAMPLE_T
    newb_ref[0:keep_b] = hb_ref[SAMPLE_T:CONV_B - 1]
    newb_ref[keep_b:CONV_B - 1] = glu.reshape(SAMPLE_T, nseq, D_B)
    newc_ref[...] = cx.reshape(SAMPLE_T, nseq, D_C)[SAMPLE_T - (CONV_C - 1):]

    yb = jax.nn.silu(_layernorm(yb, blng_ref[...], blnb_ref[...]))
    yc = zc_b * conv_c
    out_ref[...] = _stage_out(ya, yb, yc, x, wout_ref)


def _mixer_sample(x, p, layer, nseq):
    rows = SAMPLE_T * nseq
    spec = functools.partial(_layer_spec, layer)

    def full(shape):
        return pl.BlockSpec(shape, lambda i: (0,) * len(shape))

    return pl.pallas_call(
        functools.partial(_mixer_sample_kernel, nseq),
        grid=(1,),
        in_specs=[
            full((rows, D_MODEL)),
            spec((1, D_MODEL)),
            spec((D_MODEL, D_IN)),
            spec((1, D_A)),
            spec((1, D_A)),
            spec((SAMPLE_T, SAMPLE_T, D_A)),
            spec((SAMPLE_T, D_A)),
            spec((CONV_B, D_B)),
            spec((1, D_B)),
            spec((1, D_B)),
            spec((1, D_B)),
            spec((CONV_C, D_C)),
            spec((D_MODEL, D_MODEL)),
            spec((CONV_B - 1, nseq, D_B)),
            spec((CONV_C - 1, nseq, D_C)),
        ],
        out_specs=[
            full((rows, D_MODEL)),
            full((CONV_B - 1, nseq, D_B)),
            full((CONV_C - 1, nseq, D_C)),
            full((rows, D_A)),
        ],
        out_shape=[
            jax.ShapeDtypeStruct((rows, D_MODEL), _F32),
            jax.ShapeDtypeStruct((CONV_B - 1, nseq, D_B), _F32),
            jax.ShapeDtypeStruct((CONV_C - 1, nseq, D_C), _F32),
            jax.ShapeDtypeStruct((rows, D_A), _F32),
        ],
        compiler_params=pltpu.CompilerParams(
            dimension_semantics=("arbitrary",),
            vmem_limit_bytes=VMEM_LIMIT_BYTES),
        name="mixer_sample",
    )(x, p["n1g"], p["win"], p["alng"], p["alnb"], p["coef_s"], p["abias_s"], p["bcw"], p["bcb"],
      p["blng"], p["blnb"], p["ccw"], p["wout"], p["hist_b"], p["hist_c"])


def _ffn_kernel(final, n_sample_tiles, xs_ref, xp_ref, g_ref, w1_ref, w2_ref, gf_ref,
                outs_ref, outp_ref):
    i = pl.program_id(0)
    x = jnp.where(i < n_sample_tiles, xs_ref[...], xp_ref[...])
    h = _rmsnorm(x, g_ref[...]).astype(_BF16)
    acc = x
    for j in range(D_FF // FF_CHUNK):
        cols = slice(j * FF_CHUNK, (j + 1) * FF_CHUNK)
        a = jnp.dot(h, w1_ref[:, cols], preferred_element_type=_F32)
        a = jnp.square(jnp.maximum(a, 0.0)).astype(_BF16)
        acc = acc + jnp.dot(a, w2_ref[cols, :], preferred_element_type=_F32)
    if final:
        acc = _rmsnorm(acc, gf_ref[...])
    outs_ref[...] = acc
    outp_ref[...] = acc


def _ffn(xs, xp, p, layer, final, sample_rows):
    tm = ROW_TILE
    ns = sample_rows // tm
    n_prompt = xp.shape[0] // tm
    spec = functools.partial(_layer_spec, layer)
    sample_in = pl.BlockSpec((tm, D_MODEL), lambda i: (jnp.minimum(i, ns - 1), 0))
    sample_out = pl.BlockSpec((tm, D_MODEL), lambda i: (jnp.minimum(i, ns), 0))
    prompt = pl.BlockSpec((tm, D_MODEL), lambda i: (jnp.maximum(i - ns, 0), 0))
    return pl.pallas_call(
        functools.partial(_ffn_kernel, final, ns),
        grid=(ns + n_prompt,),
        in_specs=[
            sample_in,
            prompt,
            spec((1, D_MODEL)),
            spec((D_MODEL, D_FF)),
            spec((D_FF, D_MODEL)),
            pl.BlockSpec((1, D_MODEL), lambda i: (0, 0), pipeline_mode=pl.Buffered(1)),
        ],
        out_specs=[sample_out, prompt],
        out_shape=[
            jax.ShapeDtypeStruct((sample_rows + tm, D_MODEL), _F32),
            jax.ShapeDtypeStruct(xp.shape, _F32),
        ],
        compiler_params=pltpu.CompilerParams(
            dimension_semantics=("arbitrary",),
            vmem_limit_bytes=VMEM_LIMIT_BYTES),
        name="ffn_final" if final else "ffn",
    )(xs, xp, p["g2"], p["w1"], p["w2"], p["gf"])


def kernel(x_prompt, x_sample, state_conv_b, state_conv_c, norm1_g, w_in, a_ln_g, a_ln_b, a_ws,
           a_bias, b_conv_w, b_conv_b, b_ln_g, b_ln_b, c_conv_w, w_out, norm2_g, w_ff1, w_ff2,
           norm_f_g):
    depth = w_in.shape[0]
    batch, seq, _ = x_prompt.shape
    dec_batch, dec_seq, _ = x_sample.shape
    dec_rows = dec_batch * dec_seq
    assert dec_seq == SAMPLE_T and seq % ROW_TILE == 0 and dec_rows % ROW_TILE == 0

    bias_pos = jnp.transpose(a_bias, (0, 2, 1))
    p = {
        "n1g": norm1_g.reshape(depth, 1, D_MODEL),
        "win": w_in.astype(_BF16),
        "alng": a_ln_g.reshape(depth, 1, D_A),
        "alnb": a_ln_b.reshape(depth, 1, D_A),
        "aw": a_ws,
        "abias": jnp.repeat(bias_pos, A_HEAD_DIM, axis=2),
        "abias_s": jnp.repeat(bias_pos[:, :SAMPLE_T], A_HEAD_DIM, axis=2),
        "coef_s": jnp.repeat(jnp.transpose(a_ws[:, :, :SAMPLE_T, :SAMPLE_T], (0, 2, 3, 1)),
                             A_HEAD_DIM, axis=3),
        "bcw": b_conv_w,
        "bcb": b_conv_b.reshape(depth, 1, D_B),
        "blng": b_ln_g.reshape(depth, 1, D_B),
        "blnb": b_ln_b.reshape(depth, 1, D_B),
        "ccw": c_conv_w,
        "wout": w_out.astype(_BF16),
        "g2": norm2_g.reshape(depth, 1, D_MODEL),
        "w1": w_ff1.astype(_BF16),
        "w2": w_ff2.astype(_BF16),
        "gf": norm_f_g.reshape(1, D_MODEL),
        "hist_b": jnp.transpose(state_conv_b, (0, 2, 1, 3)),
        "hist_c": jnp.transpose(state_conv_c, (0, 2, 1, 3)),
    }

    xp = x_prompt.reshape(batch * seq, D_MODEL)
    xs = jnp.transpose(x_sample, (1, 0, 2)).reshape(dec_rows, D_MODEL)
    cb_p, cc_p, cb_s, cc_s, v_s = [], [], [], [], []
    for l in range(depth):
        final = l == depth - 1
        xp, nbp, ncp = _mixer_prompt(xp, p, l, seq)
        xs, nbs, ncs, vs = _mixer_sample(xs, p, l, dec_batch)
        xs, xp = _ffn(xs, xp, p, l, final, dec_rows)
        cb_p.append(nbp); cc_p.append(ncp); cb_s.append(nbs); cc_s.append(ncs); v_s.append(vs)

    def sequence_major(a):
        return jnp.transpose(a, (0, 2, 1, 3))

    y_sample = jnp.transpose(xs[:dec_rows].reshape(dec_seq, dec_batch, D_MODEL), (1, 0, 2))
    new_v = sequence_major(jnp.stack(v_s).reshape(depth, dec_seq, dec_batch, D_A))
    return (xp.reshape(batch, seq, D_MODEL), y_sample, jnp.stack(cb_p), jnp.stack(cc_p),
            sequence_major(jnp.stack(cb_s)), sequence_major(jnp.stack(cc_s)), new_v)
```
